```python
import jax, jax.numpy as jnp
from jax import lax
import numpy as np

D_MODEL = 2048
BATCH = 8
SEQ = 4096
DEPTH = 4

GRID_W = 64
ROPE_THETA = 10000.0
EPS = 1e-6
BLOCK_Q = 128

RET_HEADS = 8
RET_DK = 64
RET_DV = 128
RET_CHUNK = 128

GQA_HEADS = 8
GQA_KV_HEADS = 2
GQA_DH = 128

MLA_HEADS = 8
MLA_Q_RANK = 512
MLA_KV_RANK = 256
MLA_NOPE = 128
MLA_ROPE = 64
MLA_QK = MLA_NOPE + MLA_ROPE
MLA_DV = 128

N_BRANCH = 3
BRANCH_W = 1024

PEER_HEADS = 8
PEER_NKEYS = 128
PEER_N = PEER_NKEYS * PEER_NKEYS
PEER_DKEY = 256
PEER_DHALF = PEER_DKEY // 2
PEER_TOPK = 16
PEER_CHUNK = 128

RET_Q_W = RET_HEADS * RET_DK
RET_V_W = RET_HEADS * RET_DV
GQA_Q_W = GQA_HEADS * GQA_DH
GQA_KV_W = GQA_KV_HEADS * GQA_DH
IN_SPLITS = (RET_Q_W, RET_Q_W, RET_V_W, RET_V_W,
             GQA_Q_W, GQA_KV_W, GQA_KV_W,
             MLA_Q_RANK, MLA_KV_RANK, MLA_ROPE,
             N_BRANCH * D_MODEL)
D_IN = 2 * RET_Q_W + 2 * RET_V_W + GQA_Q_W + 2 * GQA_KV_W + MLA_Q_RANK + MLA_KV_RANK + MLA_ROPE + N_BRANCH * D_MODEL

kernel_name = "hybrid_retention_gqa_mla_peer_encoder"


def rms(x):
    xf = x.astype(jnp.float32)
    return (xf * lax.rsqrt(jnp.mean(xf * xf, -1, keepdims=True) + EPS)).astype(x.dtype)


def rotate(x, cos, sin):
    d2 = x.shape[-1] // 2
    c = cos[:, None, :].astype(x.dtype)
    s = sin[:, None, :].astype(x.dtype)
    x1, x2 = x[..., :d2], x[..., d2:]
    return jnp.concatenate([x1 * c - x2 * s, x1 * s + x2 * c], -1)


def rope_1d(seq, dim):
    half = dim // 2
    inv = ROPE_THETA ** (-jnp.arange(half, dtype=jnp.float32) / half)
    ang = jnp.arange(seq, dtype=jnp.float32)[:, None] * inv
    return jnp.cos(ang), jnp.sin(ang)


def rope_axial(seq, dim):
    rows = seq // GRID_W
    row = jnp.repeat(jnp.arange(rows), GRID_W).astype(jnp.float32)
    col = jnp.tile(jnp.arange(GRID_W), rows).astype(jnp.float32)
    quarter = dim // 4
    inv = ROPE_THETA ** (-jnp.arange(quarter, dtype=jnp.float32) / quarter)
    ang = jnp.concatenate([row[:, None] * inv, col[:, None] * inv], -1)
    return jnp.cos(ang), jnp.sin(ang)


def retention_scan(q, k, v, log_gamma):
    B, S, H, dk = q.shape
    dv = v.shape[-1]
    C = RET_CHUNK
    N = S // C
    dt = q.dtype
    qc = q.reshape(B, N, C, H, dk).transpose(1, 0, 3, 2, 4)
    kc = k.reshape(B, N, C, H, dk).transpose(1, 0, 3, 2, 4)
    vc = v.reshape(B, N, C, H, dv).transpose(1, 0, 3, 2, 4)
    idx = jnp.arange(C, dtype=jnp.float32)
    diff = idx[:, None] - idx[None, :]
    lg = log_gamma[:, None, None]
    intra_decay = jnp.where(diff >= 0, jnp.exp(jnp.maximum(diff, 0.0) * lg), 0.0).astype(dt)
    q_decay = jnp.exp((idx[None, :] + 1.0) * log_gamma[:, None]).astype(dt)
    k_decay = jnp.exp((C - 1.0 - idx[None, :]) * log_gamma[:, None]).astype(dt)
    chunk_decay = jnp.exp(C * log_gamma).astype(dt)

    def step(state, inp):
        qi, ki, vi = inp
        scores = jnp.einsum('bhid,bhjd->bhij', qi, ki) * intra_decay
        o = jnp.einsum('bhij,bhjv->bhiv', scores, vi) + jnp.einsum(
            'bhid,bhdv->bhiv', qi * q_decay[None, :, :, None], state)
        state = state * chunk_decay[None, :, None, None] + jnp.einsum(
            'bhjd,bhjv->bhdv', ki * k_decay[None, :, :, None], vi)
        return state, o

    state0 = jnp.zeros((B, H, dk, dv), dt)
    _, o = lax.scan(step, state0, (qc, kc, vc))
    return o.transpose(1, 0, 3, 2, 4).reshape(B, S, H, dv)


def block_attention(q, k, v, scale):
    B, S, H, d = q.shape
    Hkv = k.shape[2]
    G = H // Hkv
    dv = v.shape[-1]
    nb = S // BLOCK_Q
    qb = q.reshape(B, nb, BLOCK_Q, Hkv, G, d).transpose(1, 0, 2, 3, 4, 5)

    def one(qi):
        s = jnp.einsum('bqkgd,bskd->bkgqs', qi, k).astype(jnp.float32) * scale
        p = jax.nn.softmax(s, axis=-1).astype(v.dtype)
        return jnp.einsum('bkgqs,bskv->bqkgv', p, v)

    o = lax.map(one, qb)
    return o.transpose(1, 0, 2, 3, 4, 5).reshape(B, S, H, dv)


def hybrid_mixer(h, w_in, ret_decay, ret_gn, gqa_qn, gqa_kn, mla_q_ln, mla_kv_ln,
                 w_q_b, w_kv_b, mla_qn, mla_kn, w_branch, w_out):
    B, S, _ = h.shape
    z = h @ w_in
    split_points = np.cumsum(IN_SPLITS)[:-1].tolist()
    rq, rk, rv, rg, gq, gk, gv, mq, mkv, mkr, gl = jnp.split(z, split_points, axis=-1)

    cos1, sin1 = rope_1d(S, RET_DK)
    rq = rotate(rq.reshape(B, S, RET_HEADS, RET_DK), cos1, sin1)
    rk = rotate(rk.reshape(B, S, RET_HEADS, RET_DK), cos1, sin1) * (RET_DK ** -0.5)
    rv = rv.reshape(B, S, RET_HEADS, RET_DV)
    log_g = jnp.log1p(-jnp.exp(ret_decay.astype(jnp.float32)))
    fwd = retention_scan(rq, rk, rv, log_g[0])
    bwd = retention_scan(rq[:, ::-1], rk[:, ::-1], rv[:, ::-1], log_g[1])[:, ::-1]
    oa = rms(fwd + bwd) * ret_gn
    oa = jax.nn.silu(rg) * oa.reshape(B, S, RET_V_W)

    cos2, sin2 = rope_axial(S, GQA_DH)
    gq = rotate(rms(gq.reshape(B, S, GQA_HEADS, GQA_DH)) * gqa_qn, cos2, sin2)
    gk = rotate(rms(gk.reshape(B, S, GQA_KV_HEADS, GQA_DH)) * gqa_kn, cos2, sin2)
    gv = gv.reshape(B, S, GQA_KV_HEADS, GQA_DH)
    ob = block_attention(gq, gk, gv, GQA_DH ** -0.5).reshape(B, S, GQA_Q_W)

    cq = rms(mq) * mla_q_ln
    q = (cq @ w_q_b).reshape(B, S, MLA_HEADS, MLA_QK)
    ckv = rms(mkv) * mla_kv_ln
    kv = (ckv @ w_kv_b).reshape(B, S, MLA_HEADS, MLA_NOPE + MLA_DV)
    k_nope, vc = kv[..., :MLA_NOPE], kv[..., MLA_NOPE:]
    k_rope = jnp.broadcast_to(mkr[:, :, None, :], (B, S, MLA_HEADS, MLA_ROPE))
    k = jnp.concatenate([k_nope, k_rope], -1)
    q = rms(q) * mla_qn
    k = rms(k) * mla_kn
    cos3, sin3 = rope_axial(S, MLA_ROPE)
    q = jnp.concatenate([q[..., :MLA_NOPE], rotate(q[..., MLA_NOPE:], cos3, sin3)], -1)
    k = jnp.concatenate([k[..., :MLA_NOPE], rotate(k[..., MLA_NOPE:], cos3, sin3)], -1)
    oc = block_attention(q, k, vc, MLA_QK ** -0.5).reshape(B, S, MLA_HEADS * MLA_DV)

    gates = jax.nn.sigmoid(gl.reshape(B, S, N_BRANCH, D_MODEL))
    y = (gates[:, :, 0] * (oa @ w_branch[0])
         + gates[:, :, 1] * (ob @ w_branch[1])
         + gates[:, :, 2] * (oc @ w_branch[2]))
    return y @ w_out


def peer(h, w_query, sub_keys, u_emb, v_emb):
    B, S, D = h.shape
    q = rms((h @ w_query).reshape(B, S, PEER_HEADS, 2, PEER_DHALF))
    s = jnp.einsum('bshpd,hpnd->bshpn', q, sub_keys).astype(jnp.float32)
    top_s, top_i = lax.top_k(s, PEER_TOPK)
    cand = top_s[..., 0, :, None] + top_s[..., 1, None, :]
    cand_i = top_i[..., 0, :, None] * PEER_NKEYS + top_i[..., 1, None, :]
    cand = cand.reshape(B, S, PEER_HEADS, PEER_TOPK * PEER_TOPK)
    cand_i = cand_i.reshape(B, S, PEER_HEADS, PEER_TOPK * PEER_TOPK)
    best_s, pos = lax.top_k(cand, PEER_TOPK)
    idx = jnp.take_along_axis(cand_i, pos, axis=-1)
    g = jax.nn.softmax(best_s, axis=-1).astype(h.dtype)

    T = B * S
    nchunk = T // PEER_CHUNK
    hc = h.reshape(nchunk, PEER_CHUNK, D)
    ic = idx.reshape(nchunk, PEER_CHUNK, PEER_HEADS * PEER_TOPK)
    gc = g.reshape(nchunk, PEER_CHUNK, PEER_HEADS * PEER_TOPK)

    def one(args):
        hi, ii, gi = args
        u = jnp.take(u_emb, ii, axis=0)
        a = jax.nn.gelu(jnp.einsum('cd,ced->ce', hi, u))
        v = jnp.take(v_emb, ii, axis=0)
        return jnp.einsum('ce,ced->cd', gi * a, v)

    return lax.map(one, (hc, ic, gc)).reshape(B, S, D)


def setup_inputs(seed: int = 0) -> dict:
    key = jax.random.key(seed)
    ks = jax.random.split(key, 24)
    L, D = DEPTH, D_MODEL
    f32 = jnp.float32

    def nrm(k, shape, scale):
        return jax.random.normal(k, shape, f32) * scale

    def gain(k, shape):
        return 1.0 + 0.02 * jax.random.normal(k, shape, f32)

    base_decay = jnp.log(2.0 ** (-5.0 - jnp.arange(RET_HEADS, dtype=f32)))
    ret_decay = base_decay[None, None, :] + 0.05 * jax.random.normal(ks[7], (L, 2, RET_HEADS), f32)
    return {
        "x": nrm(ks[0], (BATCH, SEQ, D), 1.0),
        "c": nrm(ks[1], (BATCH, D), 1.0),
        "w_ada": nrm(ks[2], (L, D, 6 * D), 0.5 * D ** -0.5),
        "b_ada": nrm(ks[3], (L, 6 * D), 0.02),
        "norm1": gain(ks[4], (L, D)),
        "norm2": gain(ks[5], (L, D)),
        "w_in": nrm(ks[6], (L, D, D_IN), D ** -0.5),
        "ret_decay": ret_decay,
        "ret_gn": gain(ks[8], (L, RET_HEADS, RET_DV)),
        "gqa_qn": gain(ks[9], (L, GQA_DH)),
        "gqa_kn": gain(ks[10], (L, GQA_DH)),
        "mla_q_ln": gain(ks[11], (L, MLA_Q_RANK)),
        "mla_kv_ln": gain(ks[12], (L, MLA_KV_RANK)),
        "w_q_b": nrm(ks[13], (L, MLA_Q_RANK, MLA_HEADS * MLA_QK), MLA_Q_RANK ** -0.5),
        "w_kv_b": nrm(ks[14], (L, MLA_KV_RANK, MLA_HEADS * (MLA_NOPE + MLA_DV)), MLA_KV_RANK ** -0.5),
        "mla_qn": gain(ks[15], (L, MLA_QK)),
        "mla_kn": gain(ks[16], (L, MLA_QK)),
        "w_branch": nrm(ks[17], (L, N_BRANCH, BRANCH_W, D), BRANCH_W ** -0.5),
        "w_out": nrm(ks[18], (L, D, D), D ** -0.5),
        "peer_wq": nrm(ks[19], (L, D, PEER_HEADS * PEER_DKEY), D ** -0.5),
        "peer_keys": nrm(ks[20], (L, PEER_HEADS, 2, PEER_NKEYS, PEER_DHALF), PEER_DHALF ** -0.5),
        "peer_u": nrm(ks[21], (L, PEER_N, D), D ** -0.5),
        "peer_v": nrm(ks[22], (L, PEER_N, D), 0.5),
    }


def reference(x, c, w_ada, b_ada, norm1, norm2, w_in, ret_decay, ret_gn, gqa_qn, gqa_kn,
              mla_q_ln, mla_kv_ln, w_q_b, w_kv_b, mla_qn, mla_kn, w_branch, w_out,
              peer_wq, peer_keys, peer_u, peer_v):
    sc = jax.nn.silu(c)
    for l in range(DEPTH):
        mod = sc @ w_ada[l] + b_ada[l]
        sh1, sc1, g1, sh2, sc2, g2 = jnp.split(mod, 6, axis=-1)
        h = rms(x) * norm1[l] * (1.0 + sc1[:, None, :]) + sh1[:, None, :]
        x = x + g1[:, None, :] * hybrid_mixer(
            h, w_in[l], ret_decay[l], ret_gn[l], gqa_qn[l], gqa_kn[l], mla_q_ln[l], mla_kv_ln[l],
            w_q_b[l], w_kv_b[l], mla_qn[l], mla_kn[l], w_branch[l], w_out[l])
        h = rms(x) * norm2[l] * (1.0 + sc2[:, None, :]) + sh2[:, None, :]
        x = x + g2[:, None, :] * peer(h, peer_wq[l], peer_keys[l], peer_u[l], peer_v[l])
    return x
```

```python
import functools

import jax
import jax.numpy as jnp
import numpy as np
from jax import lax
from jax.experimental import pallas as pl
from jax.experimental.pallas import tpu as pltpu

F32 = jnp.float32
BF16 = jnp.bfloat16

GRID_W = 64
ROPE_THETA = 10000.0
EPS = 1e-6
RET_HEADS, RET_DK, RET_DV, RET_CHUNK = 8, 64, 128, 128
GQA_HEADS, GQA_KV_HEADS, GQA_DH = 8, 2, 128
MLA_HEADS, MLA_Q_RANK, MLA_KV_RANK, MLA_NOPE, MLA_ROPE, MLA_DV = 8, 512, 256, 128, 64, 128
MLA_QK = MLA_NOPE + MLA_ROPE
MLA_PAD = 256
N_BRANCH, BRANCH_W = 3, 1024
PEER_HEADS, PEER_NKEYS, PEER_DHALF, PEER_TOPK = 8, 128, 128, 16
PEER_N = PEER_NKEYS * PEER_NKEYS
NEG = -1e30

LANE = 128
VMEM_LIMIT = 56 * 1024 * 1024


def _params(sem):
    return pltpu.CompilerParams(dimension_semantics=sem, vmem_limit_bytes=VMEM_LIMIT)


def _tile(n, pref):
    if n <= pref:
        return n
    t = (pref // LANE) * LANE
    while t > LANE and n % t:
        t -= LANE
    assert n % t == 0, (n, pref)
    return t


def _mod_kernel(c_ref, w_ref, b_ref, o_ref):
    c = c_ref[...]
    sc = c * (1.0 / (1.0 + jnp.exp(-c)))
    o_ref[0] = jnp.dot(sc, w_ref[0], preferred_element_type=F32) + b_ref[0]


def adaln_mod(c, w_ada, b_ada):
    L, D, N = w_ada.shape
    B = c.shape[0]
    tn = _tile(N, 1024)
    return pl.pallas_call(
        _mod_kernel,
        grid=(L, N // tn),
        in_specs=[pl.BlockSpec((B, D), lambda l, j: (0, 0)),
                  pl.BlockSpec((1, D, tn), lambda l, j: (l, 0, j)),
                  pl.BlockSpec((1, 1, tn), lambda l, j: (l, 0, j))],
        out_specs=pl.BlockSpec((1, B, tn), lambda l, j: (l, 0, j)),
        out_shape=jax.ShapeDtypeStruct((L, B, N), F32),
        compiler_params=_params(("parallel", "parallel")),
        name="adaln_mod",
    )(c, w_ada, b_ada.reshape(L, 1, N))


def _norm_kernel(x_ref, g_ref, m_ref, o_ref, *ot_ref, row_shift, row_scale):
    x = x_ref[0]
    r = lax.rsqrt(jnp.mean(x * x, axis=-1, keepdims=True) + EPS)
    m = m_ref[0]
    h = x * r * g_ref[...] * (1.0 + m[row_scale:row_scale + 1]) + m[row_shift:row_shift + 1]
    o_ref[0] = h.astype(BF16)
    if ot_ref:
        ot_ref[0][0] = h.T.astype(BF16)


def mod_norm(x3, gain, mod_l, row_shift, row_scale, with_transpose=False):
    B, S, D = x3.shape
    ts = _tile(S, 256)
    out_shape = [jax.ShapeDtypeStruct((B, S, D), BF16)]
    out_specs = [pl.BlockSpec((1, ts, D), lambda b, i: (b, i, 0))]
    if with_transpose:
        out_shape.append(jax.ShapeDtypeStruct((B, D, S), BF16))
        out_specs.append(pl.BlockSpec((1, D, ts), lambda b, i: (b, 0, i)))
    res = pl.pallas_call(
        functools.partial(_norm_kernel, row_shift=row_shift, row_scale=row_scale),
        grid=(B, S // ts),
        in_specs=[pl.BlockSpec((1, ts, D), lambda b, i: (b, i, 0)),
                  pl.BlockSpec((1, D), lambda b, i: (0, 0)),
                  pl.BlockSpec((1, 6, D), lambda b, i: (b, 0, 0))],
        out_specs=out_specs,
        out_shape=out_shape,
        compiler_params=_params(("parallel", "parallel")),
        name="mod_norm_t" if with_transpose else "mod_norm",
    )(x3, gain.reshape(1, D), mod_l)
    return res if with_transpose else res[0]


def _mm_kernel(a_ref, b_ref, o_ref):
    o_ref[...] = jnp.dot(a_ref[...], b_ref[...], preferred_element_type=F32).astype(o_ref.dtype)


def matmul(a, b, out_dtype=BF16, tm=1024, tn=896):
    M, K = a.shape
    N = b.shape[1]
    tm, tn = _tile(M, tm), _tile(N, tn)
    return pl.pallas_call(
        _mm_kernel,
        grid=(M // tm, N // tn),
        in_specs=[pl.BlockSpec((tm, K), lambda i, j: (i, 0)),
                  pl.BlockSpec((K, tn), lambda i, j: (0, j))],
        out_specs=pl.BlockSpec((tm, tn), lambda i, j: (i, j)),
        out_shape=jax.ShapeDtypeStruct((M, N), out_dtype),
        compiler_params=_params(("parallel", "parallel")),
        name="matmul",
    )(a, b)


def _mm_res_kernel(a_ref, b_ref, x_ref, m_ref, o_ref, *, row_gate):
    acc = jnp.dot(a_ref[0], b_ref[...], preferred_element_type=F32)
    o_ref[0] = x_ref[0] + m_ref[0][row_gate:row_gate + 1] * acc


def matmul_residual(a3, b, x3, mod_l, row_gate, tm=1024, tn=512):
    B, S, K = a3.shape
    N = b.shape[1]
    tm, tn = _tile(S, tm), _tile(N, tn)
    return pl.pallas_call(
        functools.partial(_mm_res_kernel, row_gate=row_gate),
        grid=(B, S // tm, N // tn),
        in_specs=[pl.BlockSpec((1, tm, K), lambda b_, i, j: (b_, i, 0)),
                  pl.BlockSpec((K, tn), lambda b_, i, j: (0, j)),
                  pl.BlockSpec((1, tm, tn), lambda b_, i, j: (b_, i, j)),
                  pl.BlockSpec((1, 6, tn), lambda b_, i, j: (b_, 0, j))],
        out_specs=pl.BlockSpec((1, tm, tn), lambda b_, i, j: (b_, i, j)),
        out_shape=jax.ShapeDtypeStruct((B, S, N), F32),
        compiler_params=_params(("parallel", "parallel", "parallel")),
        name="matmul_residual",
    )(a3, b, x3, mod_l)


def _mm_grouprms_kernel(a_ref, b_ref, o_ref):
    acc = jnp.dot(a_ref[...], b_ref[...], preferred_element_type=F32)
    for g in range(acc.shape[1] // LANE):
        v = acc[:, g * LANE:(g + 1) * LANE]
        r = lax.rsqrt(jnp.mean(v * v, axis=-1, keepdims=True) + EPS)
        o_ref[:, g * LANE:(g + 1) * LANE] = (v * r).astype(BF16)


def matmul_grouprms(a, b, tm=1024, tn=512):
    M, K = a.shape
    N = b.shape[1]
    tm, tn = _tile(M, tm), _tile(N, tn)
    return pl.pallas_call(
        _mm_grouprms_kernel,
        grid=(M // tm, N // tn),
        in_specs=[pl.BlockSpec((tm, K), lambda i, j: (i, 0)),
                  pl.BlockSpec((K, tn), lambda i, j: (0, j))],
        out_specs=pl.BlockSpec((tm, tn), lambda i, j: (i, j)),
        out_shape=jax.ShapeDtypeStruct((M, N), BF16),
        compiler_params=_params(("parallel", "parallel")),
        name="matmul_grouprms",
    )(a, b)


def _pair_tables(cos, sin):
    lane = np.arange(LANE)
    idx = lane % 32
    low = (lane % 64) < 32
    c = cos[:, idx]
    s = sin[:, idx]
    return c, jnp.where(low, -s, 0.0), jnp.where(low, 0.0, s)


def _full_tables(cos, sin):
    lane = np.arange(LANE)
    idx = lane % 64
    return cos[:, idx], jnp.where(lane < 64, -sin[:, idx], sin[:, idx])


def _rope_1d(seq, dim):
    half = dim // 2
    inv = ROPE_THETA ** (-jnp.arange(half, dtype=F32) / half)
    ang = jnp.arange(seq, dtype=F32)[:, None] * inv
    return jnp.cos(ang), jnp.sin(ang)


def _rope_axial(seq, dim):
    rows = seq // GRID_W
    row = jnp.repeat(jnp.arange(rows), GRID_W).astype(F32)
    col = jnp.tile(jnp.arange(GRID_W), rows).astype(F32)
    quarter = dim // 4
    inv = ROPE_THETA ** (-jnp.arange(quarter, dtype=F32) / quarter)
    ang = jnp.concatenate([row[:, None] * inv, col[:, None] * inv], -1)
    return jnp.cos(ang), jnp.sin(ang)


def _rope_pair(x, c, sa, sb):
    return x * c + pltpu.roll(x, 96, 1) * sa + pltpu.roll(x, 32, 1) * sb


def _ret_kernel(lg_ref, q_ref, k_ref, v_ref, g_ref, c_ref, sa_ref, sb_ref, gn_ref, o_ref,
                qr_ref, kr_ref, acc_ref):
    S = q_ref.shape[1]
    C = RET_CHUNK
    n_chunks = S // C
    hp = pl.program_id(1)

    def prep(n, carry):
        sl = pl.ds(pl.multiple_of(n * C, C), C)
        c, sa, sb = c_ref[sl, :], sa_ref[sl, :], sb_ref[sl, :]
        qr_ref[sl, :] = _rope_pair(q_ref[0, sl, :].astype(F32), c, sa, sb)
        kr_ref[sl, :] = _rope_pair(k_ref[0, sl, :].astype(F32), c, sa, sb) * (RET_DK ** -0.5)
        return carry

    lax.fori_loop(0, n_chunks, prep, 0)

    ii = lax.broadcasted_iota(jnp.int32, (C, C), 0).astype(F32)
    jj = lax.broadcasted_iota(jnp.int32, (C, C), 1).astype(F32)
    diff = ii - jj
    pos = lax.broadcasted_iota(jnp.int32, (C, RET_DK), 0).astype(F32)
    ones_row = jnp.ones((1, RET_DV), F32)

    dmat, qdf, kdf, cdf, qdb, kdb, cdb = [], [], [], [], [], [], []
    for hh in range(2):
        lgf = lg_ref[0, hp * 2 + hh]
        lgb = lg_ref[1, hp * 2 + hh]
        dmat.append(jnp.where(diff >= 0, jnp.exp(jnp.maximum(diff, 0.0) * lgf), 0.0)
                    + jnp.where(diff <= 0, jnp.exp(jnp.maximum(-diff, 0.0) * lgb), 0.0))
        qdf.append(jnp.exp((pos + 1.0) * lgf))
        kdf.append(jnp.exp((C - 1.0 - pos) * lgf))
        cdf.append(jnp.exp(ones_row * (C * lgf)))
        qdb.append(jnp.exp((C - pos) * lgb))
        kdb.append(jnp.exp(pos * lgb))
        cdb.append(jnp.exp(ones_row * (C * lgb)))

    def head_slices(sl, hh):
        q = qr_ref[sl, hh * RET_DK:(hh + 1) * RET_DK]
        k = kr_ref[sl, hh * RET_DK:(hh + 1) * RET_DK]
        v = v_ref[0, sl, hh * RET_DV:(hh + 1) * RET_DV]
        return q, k, v

    def state_update(state, k, kdec, cdec, v):
        kt = (k * kdec).T.astype(BF16)
        return state * cdec + jnp.dot(kt, v, preferred_element_type=F32)

    def fwd(n, states):
        sl = pl.ds(pl.multiple_of(n * C, C), C)
        new = []
        for hh in range(2):
            q, k, v = head_slices(sl, hh)
            s = lax.dot_general(q.astype(BF16), k.astype(BF16), (((1,), (1,)), ((), ())),
                                preferred_element_type=F32) * dmat[hh]
            o = jnp.dot(s.astype(BF16), v, preferred_element_type=F32)
            o = o + jnp.dot((q * qdf[hh]).astype(BF16), states[hh].astype(BF16),
                            preferred_element_type=F32)
            acc_ref[sl, hh * RET_DV:(hh + 1) * RET_DV] = o
            new.append(state_update(states[hh], k, kdf[hh], cdf[hh], v))
        return tuple(new)

    zero_state = (jnp.zeros((RET_DK, RET_DV), F32), jnp.zeros((RET_DK, RET_DV), F32))
    lax.fori_loop(0, n_chunks, fwd, zero_state)

    def bwd(t, states):
        n = n_chunks - 1 - t
        sl = pl.ds(pl.multiple_of(n * C, C), C)
        new = []
        for hh in range(2):
            q, k, v = head_slices(sl, hh)
            o = acc_ref[sl, hh * RET_DV:(hh + 1) * RET_DV] + jnp.dot(
                (q * qdb[hh]).astype(BF16), states[hh].astype(BF16), preferred_element_type=F32)
            r = lax.rsqrt(jnp.mean(o * o, axis=-1, keepdims=True) + EPS)
            g = g_ref[0, sl, hh * RET_DV:(hh + 1) * RET_DV].astype(F32)
            gate = g * (1.0 / (1.0 + jnp.exp(-g)))
            o_ref[0, sl, hh * RET_DV:(hh + 1) * RET_DV] = (
                gate * (o * r * gn_ref[0, hh:hh + 1, :])).astype(BF16)
            new.append(state_update(states[hh], k, kdb[hh], cdb[hh], v))
        return tuple(new)

    lax.fori_loop(0, n_chunks, bwd, zero_state)


def retention(z3, offs, log_g, ret_gn, tabs):
    B, S, _ = z3.shape
    c, sa, sb = tabs
    qb, kb = offs["rq"] // LANE, offs["rk"] // LANE
    vb, gb = offs["rv"] // (2 * RET_DV), offs["rg"] // (2 * RET_DV)
    tab_spec = pl.BlockSpec((S, LANE), lambda b, h: (0, 0))
    return pl.pallas_call(
        _ret_kernel,
        grid=(B, RET_HEADS // 2),
        in_specs=[pl.BlockSpec(memory_space=pltpu.SMEM),
                  pl.BlockSpec((1, S, LANE), lambda b, h: (b, 0, qb + h)),
                  pl.BlockSpec((1, S, LANE), lambda b, h: (b, 0, kb + h)),
                  pl.BlockSpec((1, S, 2 * RET_DV), lambda b, h: (b, 0, vb + h)),
                  pl.BlockSpec((1, S, 2 * RET_DV), lambda b, h: (b, 0, gb + h)),
                  tab_spec, tab_spec, tab_spec,
                  pl.BlockSpec((1, 2, RET_DV), lambda b, h: (h, 0, 0))],
        out_specs=pl.BlockSpec((1, S, 2 * RET_DV), lambda b, h: (b, 0, h)),
        out_shape=jax.ShapeDtypeStruct((B, S, RET_HEADS * RET_DV), BF16),
        scratch_shapes=[pltpu.VMEM((S, LANE), F32), pltpu.VMEM((S, LANE), F32),
                        pltpu.VMEM((S, 2 * RET_DV), F32)],
        compiler_params=_params(("parallel", "parallel")),
        name="retention",
    )(log_g, z3, z3, z3, z3, c, sa, sb, ret_gn.reshape(RET_HEADS // 2, 2, RET_DV))


def _qknorm_rope_kernel(x_ref, gain_ref, c_ref, s_ref, o_ref, *, scale):
    x = x_ref[0].astype(F32)
    y = x * lax.rsqrt(jnp.mean(x * x, axis=-1, keepdims=True) + EPS) * gain_ref[...]
    o_ref[0] = ((y * c_ref[...] + pltpu.roll(y, 64, 1) * s_ref[...]) * scale).astype(BF16)


def qknorm_rope(z3, off, heads, gain, tabs, scale):
    B, S, _ = z3.shape
    ts = _tile(S, 512)
    c, s = tabs
    ob = off // LANE
    return pl.pallas_call(
        functools.partial(_qknorm_rope_kernel, scale=scale),
        grid=(B, S // ts, heads),
        in_specs=[pl.BlockSpec((1, ts, LANE), lambda b, i, h: (b, i, ob + h)),
                  pl.BlockSpec((1, LANE), lambda b, i, h: (0, 0)),
                  pl.BlockSpec((ts, LANE), lambda b, i, h: (i, 0)),
                  pl.BlockSpec((ts, LANE), lambda b, i, h: (i, 0))],
        out_specs=pl.BlockSpec((1, ts, LANE), lambda b, i, h: (b, i, h)),
        out_shape=jax.ShapeDtypeStruct((B, S, heads * LANE), BF16),
        compiler_params=_params(("parallel", "parallel", "parallel")),
        name="qknorm_rope",
    )(z3, gain.reshape(1, LANE), c, s)


def _attn_kernel(q_ref, k_ref, v_ref, o_ref, *, groups, dq, dv):
    k = k_ref[0]
    v = v_ref[0]
    for g in range(groups):
        q = q_ref[0, :, g * dq:(g + 1) * dq]
        s = lax.dot_general(q, k, (((1,), (1,)), ((), ())), preferred_element_type=F32)
        m = jnp.max(s, axis=-1, keepdims=True)
        p = jnp.exp(s - m)
        l = jnp.sum(p, axis=-1, keepdims=True)
        o = jnp.dot(p.astype(BF16), v, preferred_element_type=F32)
        o_ref[0, :, g * dv:(g + 1) * dv] = (o / l).astype(BF16)


def attention(q3, k3, v3, v_off, kv_heads, groups, dq, dv, tq=256):
    B, S, _ = q3.shape
    tq = _tile(S, tq)
    vb = v_off // dv
    return pl.pallas_call(
        functools.partial(_attn_kernel, groups=groups, dq=dq, dv=dv),
        grid=(B, kv_heads, S // tq),
        in_specs=[pl.BlockSpec((1, tq, groups * dq), lambda b, h, i: (b, i, h)),
                  pl.BlockSpec((1, S, dq), lambda b, h, i: (b, 0, h)),
                  pl.BlockSpec((1, S, dv), lambda b, h, i: (b, 0, vb + h))],
        out_specs=pl.BlockSpec((1, tq, groups * dv), lambda b, h, i: (b, i, h)),
        out_shape=jax.ShapeDtypeStruct((B, S, kv_heads * groups * dv), BF16),
        compiler_params=_params(("parallel", "parallel", "parallel")),
        name="attention",
    )(q3, k3, v3)


def _rms_rows(x, gain):
    return (x * lax.rsqrt(jnp.mean(x * x, axis=-1, keepdims=True) + EPS) * gain).astype(BF16)


def _mla_q_kernel(x_ref, ln_ref, w_ref, gn_ref, gr_ref, c_ref, sa_ref, sb_ref, o_ref, *, scale):
    cq = _rms_rows(x_ref[0].astype(F32), ln_ref[...])
    q = jnp.dot(cq, w_ref[...], preferred_element_type=F32)
    nope_w = MLA_HEADS * MLA_NOPE
    low = lax.broadcasted_iota(jnp.int32, (q.shape[0], LANE), 1) < MLA_ROPE
    c, sa, sb = c_ref[...], sa_ref[...], sb_ref[...]
    for pair in range(MLA_HEADS // 2):
        rp = q[:, nope_w + pair * LANE: nope_w + (pair + 1) * LANE]
        sq = rp * rp
        ss_even = jnp.sum(jnp.where(low, sq, 0.0), axis=-1, keepdims=True)
        ss_odd = jnp.sum(jnp.where(low, 0.0, sq), axis=-1, keepdims=True)
        roped = _rope_pair(rp * gr_ref[...], c, sa, sb)
        for hh in range(2):
            h = pair * 2 + hh
            nope = q[:, h * MLA_NOPE:(h + 1) * MLA_NOPE]
            ss = jnp.sum(nope * nope, axis=-1, keepdims=True) + (ss_odd if hh else ss_even)
            r = lax.rsqrt(ss * (1.0 / MLA_QK) + EPS) * scale
            o_ref[0, :, h * MLA_PAD: h * MLA_PAD + LANE] = (nope * r * gn_ref[...]).astype(BF16)
            rh = pltpu.roll(roped, 64, 1) if hh else roped
            o_ref[0, :, h * MLA_PAD + LANE:(h + 1) * MLA_PAD] = jnp.where(low, rh * r, 0.0).astype(BF16)


def mla_q(z3, off, ln, w_q, gain_nope, gain_rope_pair, tabs, scale):
    B, S, _ = z3.shape
    ts = _tile(S, 512)
    c, sa, sb = tabs
    tab_spec = pl.BlockSpec((ts, LANE), lambda b, i: (i, 0))
    vec = lambda n: pl.BlockSpec((1, n), lambda b, i: (0, 0))
    return pl.pallas_call(
        functools.partial(_mla_q_kernel, scale=scale),
        grid=(B, S // ts),
        in_specs=[pl.BlockSpec((1, ts, MLA_Q_RANK), lambda b, i: (b, i, off // MLA_Q_RANK)),
                  vec(MLA_Q_RANK),
                  pl.BlockSpec(w_q.shape, lambda b, i: (0, 0)),
                  vec(LANE), vec(LANE), tab_spec, tab_spec, tab_spec],
        out_specs=pl.BlockSpec((1, ts, MLA_HEADS * MLA_PAD), lambda b, i: (b, i, 0)),
        out_shape=jax.ShapeDtypeStruct((B, S, MLA_HEADS * MLA_PAD), BF16),
        compiler_params=_params(("parallel", "parallel")),
        name="mla_q",
    )(z3, ln.reshape(1, -1), w_q, gain_nope.reshape(1, LANE), gain_rope_pair.reshape(1, LANE), c, sa, sb)


def _mla_kv_kernel(x_ref, kr_ref, ln_ref, w_ref, gn_ref, gr_ref, c_ref, sa_ref, sb_ref, k_ref, v_ref):
    ckv = _rms_rows(x_ref[0].astype(F32), ln_ref[...])
    kv = jnp.dot(ckv, w_ref[...], preferred_element_type=F32)
    nope_w = MLA_HEADS * MLA_NOPE
    v_ref[0] = kv[:, nope_w:].astype(BF16)
    low = lax.broadcasted_iota(jnp.int32, (kv.shape[0], LANE), 1) < MLA_ROPE
    kr = jnp.where(low, kr_ref[0].astype(F32), 0.0)
    ss_r = jnp.sum(kr * kr, axis=-1, keepdims=True)
    roped = _rope_pair(kr * gr_ref[...], c_ref[...], sa_ref[...], sb_ref[...])
    roped = jnp.where(low, roped, 0.0)
    for h in range(MLA_HEADS):
        nope = kv[:, h * MLA_NOPE:(h + 1) * MLA_NOPE]
        r = lax.rsqrt((jnp.sum(nope * nope, axis=-1, keepdims=True) + ss_r) * (1.0 / MLA_QK) + EPS)
        k_ref[0, :, h * MLA_PAD: h * MLA_PAD + LANE] = (nope * r * gn_ref[...]).astype(BF16)
        k_ref[0, :, h * MLA_PAD + LANE:(h + 1) * MLA_PAD] = (roped * r).astype(BF16)


def mla_kv(z3, off_kv, off_kr, ln, w_kv, gain_nope, gain_rope_pad, tabs):
    B, S, _ = z3.shape
    ts = _tile(S, 512)
    c, sa, sb = tabs
    tab_spec = pl.BlockSpec((ts, LANE), lambda b, i: (i, 0))
    vec = lambda n: pl.BlockSpec((1, n), lambda b, i: (0, 0))
    return pl.pallas_call(
        _mla_kv_kernel,
        grid=(B, S // ts),
        in_specs=[pl.BlockSpec((1, ts, MLA_KV_RANK), lambda b, i: (b, i, off_kv // MLA_KV_RANK)),
                  pl.BlockSpec((1, ts, LANE), lambda b, i: (b, i, off_kr // LANE)),
                  vec(MLA_KV_RANK),
                  pl.BlockSpec(w_kv.shape, lambda b, i: (0, 0)),
                  vec(LANE), vec(LANE), tab_spec, tab_spec, tab_spec],
        out_specs=[pl.BlockSpec((1, ts, MLA_HEADS * MLA_PAD), lambda b, i: (b, i, 0)),
                   pl.BlockSpec((1, ts, MLA_HEADS * MLA_DV), lambda b, i: (b, i, 0))],
        out_shape=[jax.ShapeDtypeStruct((B, S, MLA_HEADS * MLA_PAD), BF16),
                   jax.ShapeDtypeStruct((B, S, MLA_HEADS * MLA_DV), BF16)],
        compiler_params=_params(("parallel", "parallel")),
        name="mla_kv",
    )(z3, z3, ln.reshape(1, -1), w_kv, gain_nope.reshape(1, LANE), gain_rope_pad.reshape(1, LANE), c, sa, sb)


def _merge_kernel(oa_ref, ob_ref, oc_ref, w_ref, ga_ref, gb_ref, gc_ref, o_ref):
    acc = None
    for o_k, g_k, k in ((oa_ref, ga_ref, 0), (ob_ref, gb_ref, 1), (oc_ref, gc_ref, 2)):
        g = g_k[0].astype(F32)
        t = (1.0 / (1.0 + jnp.exp(-g))) * jnp.dot(o_k[0], w_ref[k], preferred_element_type=F32)
        acc = t if acc is None else acc + t
    o_ref[0] = acc.astype(BF16)


def branch_merge(oa, ob, oc, w_branch, z3, gate_off, tm=1024, tn=512):
    B, S, W = oa.shape
    D = w_branch.shape[2]
    tm, tn = _tile(S, tm), _tile(D, tn)
    gb = gate_off // tn
    nb = D // tn
    o_spec = pl.BlockSpec((1, tm, W), lambda b, i, j: (b, i, 0))
    g_spec = lambda k: pl.BlockSpec((1, tm, tn), lambda b, i, j: (b, i, gb + k * nb + j))
    return pl.pallas_call(
        _merge_kernel,
        grid=(B, S // tm, nb),
        in_specs=[o_spec, o_spec, o_spec,
                  pl.BlockSpec((N_BRANCH, W, tn), lambda b, i, j: (0, 0, j)),
                  g_spec(0), g_spec(1), g_spec(2)],
        out_specs=pl.BlockSpec((1, tm, tn), lambda b, i, j: (b, i, j)),
        out_shape=jax.ShapeDtypeStruct((B, S, D), BF16),
        compiler_params=_params(("parallel", "parallel", "parallel")),
        name="branch_merge",
    )(oa, ob, oc, w_branch, z3, z3, z3)


N_SORT = PEER_TOPK + 1


def _extract_top(s, n):
    rows = []
    for _ in range(n):
        m = jnp.max(s, axis=0, keepdims=True)
        rows.append(m)
        s = jnp.where(s == m, NEG, s)
    return rows


def _peer_select_kernel(q_ref, keys_ref, s2_ref, e2_ref, thr_ref, e1_ref):
    q = q_ref[...]
    nt = (((1,), (1,)), ((), ()))
    s1 = lax.dot_general(keys_ref[0, 0], q[:, :PEER_DHALF], nt, preferred_element_type=F32)
    s2 = lax.dot_general(keys_ref[0, 1], q[:, PEER_DHALF:], nt, preferred_element_type=F32)
    a = _extract_top(s1, N_SORT)
    b = _extract_top(s2, N_SORT)
    tb = s1.shape[1]
    b_all = jnp.concatenate(b + [jnp.full((24 - N_SORT, tb), NEG, F32)], axis=0)
    b_top8 = b_all[:8]
    cand = jnp.concatenate([a[0] + b_all] + [a[i] + b_top8 for i in range(1, N_SORT)], axis=0)
    c = _extract_top(cand, N_SORT)
    tau = 0.5 * (c[PEER_TOPK - 1] + c[PEER_TOPK])
    z = jnp.zeros_like(c[0])
    for kk in range(PEER_TOPK):
        z = z + jnp.exp(c[kk] - c[0])
    s2_ref[0] = s2
    e2_ref[0] = jnp.exp(s2 - b[0])
    thr_ref[0] = tau - s1
    e1_ref[0] = jnp.exp(s1 - a[0]) / z


def peer_select(pq, keys, tb=512):
    T = pq.shape[0]
    tb = _tile(T, tb)
    out = jax.ShapeDtypeStruct((PEER_HEADS, PEER_NKEYS, T), F32)
    o_spec = pl.BlockSpec((1, PEER_NKEYS, tb), lambda i, h: (h, 0, i))
    return pl.pallas_call(
        _peer_select_kernel,
        grid=(T // tb, PEER_HEADS),
        in_specs=[pl.BlockSpec((tb, 2 * PEER_DHALF), lambda i, h: (i, h)),
                  pl.BlockSpec((1, 2, PEER_NKEYS, PEER_DHALF), lambda i, h: (h, 0, 0, 0))],
        out_specs=[o_spec] * 4,
        out_shape=[out] * 4,
        compiler_params=_params(("parallel", "parallel")),
        name="peer_select",
    )(pq, keys)


PEER_EB = 512


def _gelu_tanh(x):
    return 0.5 * x * (1.0 + jnp.tanh(np.sqrt(2.0 / np.pi) * (x + 0.044715 * (x * x * x))))


def _peer_dense_kernel(ht_ref, u_ref, vt_ref, s2_ref, e2_ref, thr_ref, e1_ref, x_ref, m_ref, o_ref,
                       acc_ref, wa_ref, *, row_gate):
    j = pl.program_id(2)

    @pl.when(j == 0)
    def _():
        acc_ref[...] = jnp.zeros_like(acc_ref)

    act = _gelu_tanh(jnp.dot(u_ref[...], ht_ref[0], preferred_element_type=F32))
    rows_per_step = PEER_EB // PEER_NKEYS
    for r in range(rows_per_step):
        i1 = j * rows_per_step + r
        w = None
        for h in range(PEER_HEADS):
            thr = thr_ref[h, pl.ds(i1, 1), :]
            e1 = e1_ref[h, pl.ds(i1, 1), :]
            t = jnp.where(s2_ref[h] >= thr, e2_ref[h] * e1, 0.0)
            w = t if w is None else w + t
        sl = slice(r * PEER_NKEYS, (r + 1) * PEER_NKEYS)
        wa_ref[sl, :] = (w * act[sl, :]).astype(BF16)
    acc_ref[...] += jnp.dot(vt_ref[...], wa_ref[...], preferred_element_type=F32)

    @pl.when(j == pl.num_programs(2) - 1)
    def _():
        o_ref[0] = x_ref[0] + m_ref[0][row_gate:row_gate + 1] * acc_ref[...].T


def peer_dense(ht3, u, vt, sel, x3, mod_l, row_gate, tb=512):
    B, D, S = ht3.shape
    tb = _tile(S, tb)
    nsb = S // tb
    s2, e2, thr, e1 = sel
    sel_spec = pl.BlockSpec((PEER_HEADS, PEER_NKEYS, tb), lambda b, i, j: (0, 0, b * nsb + i))
    return pl.pallas_call(
        functools.partial(_peer_dense_kernel, row_gate=row_gate),
        grid=(B, nsb, PEER_N // PEER_EB),
        in_specs=[pl.BlockSpec((1, D, tb), lambda b, i, j: (b, 0, i)),
                  pl.BlockSpec((PEER_EB, D), lambda b, i, j: (j, 0)),
                  pl.BlockSpec((D, PEER_EB), lambda b, i, j: (0, j)),
                  sel_spec, sel_spec, sel_spec, sel_spec,
                  pl.BlockSpec((1, tb, D), lambda b, i, j: (b, i, 0)),
                  pl.BlockSpec((1, 6, D), lambda b, i, j: (b, 0, 0))],
        out_specs=pl.BlockSpec((1, tb, D), lambda b, i, j: (b, i, 0)),
        out_shape=jax.ShapeDtypeStruct((B, S, D), F32),
        scratch_shapes=[pltpu.VMEM((D, tb), F32), pltpu.VMEM((PEER_EB, tb), BF16)],
        compiler_params=_params(("parallel", "parallel", "arbitrary")),
        name="peer_dense",
    )(ht3, u, vt, s2, e2, thr, e1, x3, mod_l)


def _in_proj_layout(D):
    widths = dict(gl=N_BRANCH * D, rq=RET_HEADS * RET_DK, rk=RET_HEADS * RET_DK,
                  rv=RET_HEADS * RET_DV, rg=RET_HEADS * RET_DV,
                  gq=GQA_HEADS * GQA_DH, gk=GQA_KV_HEADS * GQA_DH, gv=GQA_KV_HEADS * GQA_DH,
                  mq=MLA_Q_RANK, mkv=MLA_KV_RANK, mkr=LANE)
    offs, o = {}, 0
    for name, w in widths.items():
        offs[name] = o
        o += w
    return offs, o


def _relayout_w_in(w, D):
    ref_w = [RET_HEADS * RET_DK] * 2 + [RET_HEADS * RET_DV] * 2 + [GQA_HEADS * GQA_DH] + \
            [GQA_KV_HEADS * GQA_DH] * 2 + [MLA_Q_RANK, MLA_KV_RANK, MLA_ROPE, N_BRANCH * D]
    cuts = np.cumsum(ref_w)[:-1].tolist()
    parts = jnp.split(w, cuts, axis=1)
    pad = jnp.zeros((w.shape[0], LANE - MLA_ROPE), w.dtype)
    return jnp.concatenate([parts[-1]] + parts[:-1] + [pad], axis=1).astype(BF16)


def _relayout_w_q_b(w):
    w3 = w.reshape(w.shape[0], MLA_HEADS, MLA_QK)
    return jnp.concatenate([w3[:, :, :MLA_NOPE].reshape(w.shape[0], -1),
                            w3[:, :, MLA_NOPE:].reshape(w.shape[0], -1)], axis=1).astype(BF16)


def _relayout_w_kv_b(w):
    w3 = w.reshape(w.shape[0], MLA_HEADS, MLA_NOPE + MLA_DV)
    return jnp.concatenate([w3[:, :, :MLA_NOPE].reshape(w.shape[0], -1),
                            w3[:, :, MLA_NOPE:].reshape(w.shape[0], -1)], axis=1).astype(BF16)


def kernel(x, c, w_ada, b_ada, norm1, norm2, w_in, ret_decay, ret_gn, gqa_qn, gqa_kn, mla_q_ln, mla_kv_ln,
           w_q_b, w_kv_b, mla_qn, mla_kn, w_branch, w_out, peer_wq, peer_keys, peer_u, peer_v):
    B, S, D = x.shape
    L = w_ada.shape[0]
    T = B * S
    offs, nz = _in_proj_layout(D)

    tabs_ret = _pair_tables(*_rope_1d(S, RET_DK))
    tabs_gqa = _full_tables(*_rope_axial(S, GQA_DH))
    tabs_mla = _pair_tables(*_rope_axial(S, MLA_ROPE))
    log_g = jnp.log1p(-jnp.exp(ret_decay.astype(F32)))

    mod = adaln_mod(c, w_ada, b_ada).reshape(L, B, 6, D)

    for l in range(L):
        mod_l = mod[l]
        h = mod_norm(x, norm1[l], mod_l, row_shift=0, row_scale=1)
        z3 = matmul(h.reshape(T, D), _relayout_w_in(w_in[l], D)).reshape(B, S, nz)

        oa = retention(z3, offs, log_g[l], ret_gn[l], tabs_ret)

        gq = qknorm_rope(z3, offs["gq"], GQA_HEADS, gqa_qn[l], tabs_gqa, GQA_DH ** -0.5)
        gk = qknorm_rope(z3, offs["gk"], GQA_KV_HEADS, gqa_kn[l], tabs_gqa, 1.0)
        ob = attention(gq, gk, z3, offs["gv"], GQA_KV_HEADS, GQA_HEADS // GQA_KV_HEADS, GQA_DH, GQA_DH)

        qn_rope_pair = jnp.tile(mla_qn[l, MLA_NOPE:], 2)
        kn_rope_pad = jnp.concatenate([mla_kn[l, MLA_NOPE:], jnp.zeros((LANE - MLA_ROPE,), F32)])
        mq = mla_q(z3, offs["mq"], mla_q_ln[l], _relayout_w_q_b(w_q_b[l]), mla_qn[l, :MLA_NOPE],
                   qn_rope_pair, tabs_mla, MLA_QK ** -0.5)
        mk, mv = mla_kv(z3, offs["mkv"], offs["mkr"], mla_kv_ln[l], _relayout_w_kv_b(w_kv_b[l]),
                        mla_kn[l, :MLA_NOPE], kn_rope_pad, tabs_mla)
        oc = attention(mq, mk, mv, 0, MLA_HEADS, 1, MLA_PAD, MLA_DV)

        y = branch_merge(oa, ob, oc, w_branch[l].astype(BF16), z3, offs["gl"])
        x = matmul_residual(y, w_out[l].astype(BF16), x, mod_l, row_gate=2)

        h2, h2t = mod_norm(x, norm2[l], mod_l, row_shift=3, row_scale=4, with_transpose=True)
        pq = matmul_grouprms(h2.reshape(T, D), peer_wq[l].astype(BF16))
        sel = peer_select(pq, peer_keys[l].astype(BF16))
        x = peer_dense(h2t, peer_u[l].astype(BF16), peer_v[l].T.astype(BF16), sel, x, mod_l, row_gate=5)
    return x
```
